```python
import math
import jax, jax.numpy as jnp
from jax import lax
import numpy as np

D_MODEL = 4096
BATCH = 2
SEQ = 8192
DEPTH = 2

N_MIXERS = 2

M_HEADS = 8
M_QK_DIM = D_MODEL // 2 // M_HEADS
M_V_DIM = D_MODEL // M_HEADS
M_CHUNK = 64
M_IN = 2 * M_HEADS * M_QK_DIM + 2 * M_HEADS * M_V_DIM + 2 * M_HEADS

A_HEAD_DIM = 128
A_HEADS = D_MODEL // (2 * A_HEAD_DIM)
A_IN = 3 * A_HEADS * 2 * A_HEAD_DIM
Q_BLOCK = 128

D_FF = 7 * D_MODEL // 2
CONV_WIDTH = 3

RMS_EPS = 1e-6
SUBLN_EPS = 1e-5

kernel_name = "hybrid_mlstm_diffattn_convffn"


def rmsnorm(x, g, eps=RMS_EPS):
    xf = x.astype(jnp.float32)
    y = xf * lax.rsqrt(jnp.mean(xf * xf, axis=-1, keepdims=True) + eps)
    return (y * g.astype(jnp.float32)).astype(x.dtype)


def lambda_init_fn(layer_idx):
    return 0.8 - 0.6 * math.exp(-0.3 * layer_idx)


def alibi_slopes(n_heads):
    return jnp.exp2(-8.0 * jnp.arange(1, n_heads + 1, dtype=jnp.float32) / n_heads)


def mlstm_chunkwise(q, k, v, i_pre, f_pre):
    B, S, H, dk = q.shape
    dv = v.shape[-1]
    L = M_CHUNK
    NC = S // L

    def to_chunks(t):
        return t.astype(jnp.float32).reshape(B, NC, L, H, -1).transpose(1, 0, 3, 2, 4)

    def gate_chunks(t):
        return t.astype(jnp.float32).reshape(B, NC, L, H).transpose(1, 0, 3, 2)

    qc, kc, vc = to_chunks(q), to_chunks(k), to_chunks(v)
    ic = gate_chunks(i_pre)
    lfc = jax.nn.log_sigmoid(gate_chunks(f_pre))
    causal = jnp.tril(jnp.ones((L, L), dtype=bool))

    def step(carry, xs):
        C, n, m = carry
        qq, kk, vv, ii, lf = xs
        b = jnp.cumsum(lf, axis=-1)
        dmat = jnp.where(causal, b[..., :, None] - b[..., None, :] + ii[..., None, :], -jnp.inf)
        inter = b + m[..., None]
        m_t = jnp.maximum(inter, jnp.max(dmat, axis=-1))
        s = jnp.einsum('bhtd,bhsd->bhts', qq, kk) * jnp.exp(dmat - m_t[..., None])
        a = jnp.exp(inter - m_t)
        num = jnp.einsum('bhts,bhse->bhte', s, vv) + a[..., None] * jnp.einsum('bhtd,bhde->bhte', qq, C)
        den = jnp.sum(s, axis=-1) + a * jnp.einsum('bhtd,bhd->bht', qq, n)
        h = num / jnp.maximum(jnp.abs(den), jnp.exp(-m_t))[..., None]
        b_last = b[..., -1]
        ws = b_last[..., None] - b + ii
        m_new = jnp.maximum(b_last + m, jnp.max(ws, axis=-1))
        decay = jnp.exp(b_last + m - m_new)
        ws_e = jnp.exp(ws - m_new[..., None])
        C_new = decay[..., None, None] * C + jnp.einsum('bhs,bhsd,bhse->bhde', ws_e, kk, vv)
        n_new = decay[..., None] * n + jnp.einsum('bhs,bhsd->bhd', ws_e, kk)
        return (C_new, n_new, m_new), h

    init = (jnp.zeros((B, H, dk, dv), jnp.float32),
            jnp.zeros((B, H, dk), jnp.float32),
            jnp.zeros((B, H), jnp.float32))
    _, hs = lax.scan(step, init, (qc, kc, vc, ic, lfc))
    return hs.transpose(1, 0, 3, 2, 4).reshape(B, S, H, dv)


def mlstm_mixer(h, w_in, gate_b, w_out):
    B, S, _ = h.shape
    qk = M_HEADS * M_QK_DIM
    vd = M_HEADS * M_V_DIM
    p = h @ w_in
    q = p[..., :qk].reshape(B, S, M_HEADS, M_QK_DIM)
    k = p[..., qk:2 * qk].reshape(B, S, M_HEADS, M_QK_DIM) * (M_QK_DIM ** -0.5)
    v = p[..., 2 * qk:2 * qk + vd].reshape(B, S, M_HEADS, M_V_DIM)
    o = p[..., 2 * qk + vd:2 * qk + 2 * vd]
    g = p[..., 2 * qk + 2 * vd:] + gate_b
    hs = mlstm_chunkwise(q, k, v, g[..., :M_HEADS], g[..., M_HEADS:])
    y = jax.nn.sigmoid(o.astype(jnp.float32)) * hs.reshape(B, S, vd)
    return y.astype(h.dtype) @ w_out


def diff_attention(q, k, v, lam):
    B, S, H, _, d = q.shape
    NB = S // Q_BLOCK
    qf = q.astype(jnp.float32) * (d ** -0.5)
    kf = k.astype(jnp.float32)
    vf = v.astype(jnp.float32)
    qb = qf.reshape(B, NB, Q_BLOCK, H, 2, d).transpose(1, 0, 2, 3, 4, 5)
    slopes = alibi_slopes(H)
    kpos = jnp.arange(S)

    def one_block(args):
        qblk, bi = args
        qpos = bi * Q_BLOCK + jnp.arange(Q_BLOCK)
        dist = (qpos[:, None] - kpos[None, :]).astype(jnp.float32)
        s = jnp.einsum('bqhjd,bkhjd->bhjqk', qblk, kf) - slopes[None, :, None, None, None] * dist
        s = jnp.where(dist >= 0, s, -jnp.inf)
        p = jax.nn.softmax(s, axis=-1)
        a = p[:, :, 0] - lam * p[:, :, 1]
        return jnp.einsum('bhqk,bkhe->bqhe', a, vf)

    out = lax.map(one_block, (qb, jnp.arange(NB)))
    return out.transpose(1, 0, 2, 3, 4).reshape(B, S, H, 2 * d)


def diff_attn_mixer(h, w_in, lq1, lk1, lq2, lk2, subln_g, w_out, lambda_init):
    B, S, _ = h.shape
    hd = A_HEADS * 2 * A_HEAD_DIM
    p = h @ w_in
    q = p[..., :hd].reshape(B, S, A_HEADS, 2, A_HEAD_DIM)
    k = p[..., hd:2 * hd].reshape(B, S, A_HEADS, 2, A_HEAD_DIM)
    v = p[..., 2 * hd:].reshape(B, S, A_HEADS, 2 * A_HEAD_DIM)
    f32 = jnp.float32
    lam = (jnp.exp(jnp.sum(lq1.astype(f32) * lk1.astype(f32)))
           - jnp.exp(jnp.sum(lq2.astype(f32) * lk2.astype(f32))) + lambda_init)
    o = diff_attention(q, k, v, lam)
    o = rmsnorm(o, subln_g, SUBLN_EPS) * (1.0 - lambda_init)
    return o.reshape(B, S, hd).astype(h.dtype) @ w_out


def conv_ffn(h, w_up, conv_w, conv_b, w_down):
    u = h @ w_up
    c = u.shape[-1]
    u = lax.conv_general_dilated(u, conv_w[:, None, :], window_strides=(1,),
                                 padding=[(CONV_WIDTH - 1, 0)],
                                 dimension_numbers=('NWC', 'WIO', 'NWC'),
                                 feature_group_count=c) + conv_b
    gate, val = u[..., :D_FF], u[..., D_FF:]
    return (jax.nn.silu(gate) * val) @ w_down


def setup_inputs(seed: int = 0) -> dict:
    key = jax.random.key(seed)
    ks = iter(jax.random.split(key, 40))
    f32 = jnp.float32

    def nrm(shape, scale):
        return jax.random.normal(next(ks), shape, f32) * scale

    def gain(n):
        return 1.0 + nrm((n,), 0.02)

    def ffn_params(prefix):
        return {
            prefix + 'ffn_pre_g': gain(D_MODEL),
            prefix + 'ffn_post_g': gain(D_MODEL),
            prefix + 'w_up': nrm((D_MODEL, 2 * D_FF), D_MODEL ** -0.5),
            prefix + 'conv_w': nrm((CONV_WIDTH, 2 * D_FF), CONV_WIDTH ** -0.5),
            prefix + 'conv_b': nrm((2 * D_FF,), 0.02),
            prefix + 'w_down': nrm((D_FF, D_MODEL), D_FF ** -0.5),
        }

    d = {'x': nrm((BATCH, SEQ, D_MODEL), 1.0)}
    d['l0_mix_pre_g'] = gain(D_MODEL)
    d['l0_mix_post_g'] = gain(D_MODEL)
    d['l0_w_in'] = nrm((D_MODEL, M_IN), D_MODEL ** -0.5)
    i_bias = nrm((M_HEADS,), 0.1)
    f_bias = jnp.linspace(3.0, 6.0, M_HEADS, dtype=f32) + nrm((M_HEADS,), 0.1)
    d['l0_gate_b'] = jnp.concatenate([i_bias, f_bias])
    d['l0_w_out'] = nrm((M_HEADS * M_V_DIM, D_MODEL), (M_HEADS * M_V_DIM) ** -0.5)
    d.update(ffn_params('l0_'))
    d['l1_mix_pre_g'] = gain(D_MODEL)
    d['l1_mix_post_g'] = gain(D_MODEL)
    d['l1_w_in'] = nrm((D_MODEL, A_IN), D_MODEL ** -0.5)
    d['l1_lambda_q1'] = nrm((A_HEAD_DIM,), 0.1)
    d['l1_lambda_k1'] = nrm((A_HEAD_DIM,), 0.1)
    d['l1_lambda_q2'] = nrm((A_HEAD_DIM,), 0.1)
    d['l1_lambda_k2'] = nrm((A_HEAD_DIM,), 0.1)
    d['l1_subln_g'] = gain(2 * A_HEAD_DIM)
    d['l1_w_out'] = nrm((A_HEADS * 2 * A_HEAD_DIM, D_MODEL), (A_HEADS * 2 * A_HEAD_DIM) ** -0.5)
    d.update(ffn_params('l1_'))
    return d


def reference(x,
              l0_mix_pre_g, l0_mix_post_g, l0_w_in, l0_gate_b, l0_w_out,
              l0_ffn_pre_g, l0_ffn_post_g, l0_w_up, l0_conv_w, l0_conv_b, l0_w_down,
              l1_mix_pre_g, l1_mix_post_g, l1_w_in, l1_lambda_q1, l1_lambda_k1,
              l1_lambda_q2, l1_lambda_k2, l1_subln_g, l1_w_out,
              l1_ffn_pre_g, l1_ffn_post_g, l1_w_up, l1_conv_w, l1_conv_b, l1_w_down):
    mixers = [
        lambda t: mlstm_mixer(t, l0_w_in, l0_gate_b, l0_w_out),
        lambda t: diff_attn_mixer(t, l1_w_in, l1_lambda_q1, l1_lambda_k1, l1_lambda_q2,
                                  l1_lambda_k2, l1_subln_g, l1_w_out, lambda_init_fn(1)),
    ]
    layers = [
        (l0_mix_pre_g, l0_mix_post_g, l0_ffn_pre_g, l0_ffn_post_g,
         (l0_w_up, l0_conv_w, l0_conv_b, l0_w_down)),
        (l1_mix_pre_g, l1_mix_post_g, l1_ffn_pre_g, l1_ffn_post_g,
         (l1_w_up, l1_conv_w, l1_conv_b, l1_w_down)),
    ]
    h = x
    for i in range(DEPTH):
        mix_pre, mix_post, ffn_pre, ffn_post, ffn_w = layers[i]
        h = h + rmsnorm(mixers[i](rmsnorm(h, mix_pre)), mix_post)
        h = h + rmsnorm(conv_ffn(rmsnorm(h, ffn_pre), *ffn_w), ffn_post)
    return h
```

```python
import functools
import math

import jax
import jax.numpy as jnp
from jax import lax
from jax.experimental import pallas as pl
from jax.experimental.pallas import tpu as pltpu

D_MODEL = 4096
BATCH = 2
SEQ = 8192
TOKENS = BATCH * SEQ
M_HEADS = 8
M_QK_DIM = 256
M_V_DIM = 512
M_QK = M_HEADS * M_QK_DIM
M_V = M_HEADS * M_V_DIM
M_MAIN = 2 * M_QK + 2 * M_V
A_HEADS = 16
A_HEAD_DIM = 128
A_HD = A_HEADS * 2 * A_HEAD_DIM
D_FF = 14336
RMS_EPS = 1e-6
SUBLN_EPS = 1e-5
LAMBDA_INIT = 0.8 - 0.6 * math.exp(-0.3)

LANES = 128
SUBLANES = 8
VMEM_LIMIT_BYTES = 56 * 1024 * 1024

NORM_ROWS = 256
MM_TM = 1024
MM_TN = 1024
MM_TK = 2048
FFN_TM = 1024
FFN_TN = 512
MLSTM_CHUNK = 256
ATT_TQ = 512
ATT_TK = 512
NEG_BIG = -1e30


def _cparams(semantics):
    return pltpu.CompilerParams(dimension_semantics=semantics, vmem_limit_bytes=VMEM_LIMIT_BYTES)


def _rms(x, g, eps):
    return x * lax.rsqrt(jnp.mean(x * x, axis=-1, keepdims=True) + eps) * g


def _norm_cast_kernel(x_ref, g_ref, o_ref):
    o_ref[...] = _rms(x_ref[...], g_ref[...], RMS_EPS).astype(o_ref.dtype)


def _norm_cast(x, g):
    t, d = x.shape
    return pl.pallas_call(
        _norm_cast_kernel,
        grid=(t // NORM_ROWS,),
        in_specs=[pl.BlockSpec((NORM_ROWS, d), lambda i: (i, 0)),
                  pl.BlockSpec((1, d), lambda i: (0, 0))],
        out_specs=pl.BlockSpec((NORM_ROWS, d), lambda i: (i, 0)),
        out_shape=jax.ShapeDtypeStruct((t, d), jnp.bfloat16),
        compiler_params=_cparams(("parallel",)),
        name="norm_cast",
    )(x, g.reshape(1, d))


def _residual_norm_kernel(z_ref, h_ref, gpost_ref, gpre_ref, hout_ref, nout_ref):
    h = h_ref[...] + _rms(z_ref[...], gpost_ref[...], RMS_EPS)
    hout_ref[...] = h
    nout_ref[...] = _rms(h, gpre_ref[...], RMS_EPS).astype(nout_ref.dtype)


def _residual_norm(z, h, g_post, g_pre):
    t, d = z.shape
    row = pl.BlockSpec((NORM_ROWS, d), lambda i: (i, 0))
    vec = pl.BlockSpec((1, d), lambda i: (0, 0))
    return pl.pallas_call(
        _residual_norm_kernel,
        grid=(t // NORM_ROWS,),
        in_specs=[row, row, vec, vec],
        out_specs=[row, row],
        out_shape=[jax.ShapeDtypeStruct((t, d), jnp.float32),
                   jax.ShapeDtypeStruct((t, d), jnp.bfloat16)],
        compiler_params=_cparams(("parallel",)),
        name="residual_norm",
    )(z, h, g_post.reshape(1, d), g_pre.reshape(1, d))


def _residual_kernel(z_ref, h_ref, gpost_ref, hout_ref):
    hout_ref[...] = h_ref[...] + _rms(z_ref[...], gpost_ref[...], RMS_EPS)


def _residual(z, h, g_post):
    t, d = z.shape
    row = pl.BlockSpec((NORM_ROWS, d), lambda i: (i, 0))
    vec = pl.BlockSpec((1, d), lambda i: (0, 0))
    return pl.pallas_call(
        _residual_kernel,
        grid=(t // NORM_ROWS,),
        in_specs=[row, row, vec],
        out_specs=row,
        out_shape=jax.ShapeDtypeStruct((t, d), jnp.float32),
        compiler_params=_cparams(("parallel",)),
        name="residual",
    )(z, h, g_post.reshape(1, d))


def _mm_kernel(a_ref, w_ref, o_ref):
    o_ref[...] = jnp.dot(a_ref[...], w_ref[...],
                         preferred_element_type=jnp.float32).astype(o_ref.dtype)


def _matmul(a, w, out_dtype, tn=MM_TN, name="matmul"):
    t, k = a.shape
    n = w.shape[1]
    tn = min(tn, n)
    return pl.pallas_call(
        _mm_kernel,
        grid=(t // MM_TM, n // tn),
        in_specs=[pl.BlockSpec((MM_TM, k), lambda i, j: (i, 0)),
                  pl.BlockSpec((k, tn), lambda i, j: (0, j))],
        out_specs=pl.BlockSpec((MM_TM, tn), lambda i, j: (i, j)),
        out_shape=jax.ShapeDtypeStruct((t, n), out_dtype),
        compiler_params=_cparams(("parallel", "parallel")),
        name=name,
    )(a, w)


def _mm_acc_kernel(a_ref, w_ref, o_ref, acc_ref):
    k = pl.program_id(2)

    @pl.when(k == 0)
    def _():
        acc_ref[...] = jnp.zeros_like(acc_ref)

    acc_ref[...] += jnp.dot(a_ref[...], w_ref[...], preferred_element_type=jnp.float32)

    @pl.when(k == pl.num_programs(2) - 1)
    def _():
        o_ref[...] = acc_ref[...].astype(o_ref.dtype)


def _matmul_acc(a, w, out_dtype, name="matmul_acc"):
    t, k = a.shape
    n = w.shape[1]
    return pl.pallas_call(
        _mm_acc_kernel,
        grid=(t // MM_TM, n // MM_TN, k // MM_TK),
        in_specs=[pl.BlockSpec((MM_TM, MM_TK), lambda i, j, kk: (i, kk)),
                  pl.BlockSpec((MM_TK, MM_TN), lambda i, j, kk: (kk, j))],
        out_specs=pl.BlockSpec((MM_TM, MM_TN), lambda i, j, kk: (i, j)),
        out_shape=jax.ShapeDtypeStruct((t, n), out_dtype),
        scratch_shapes=[pltpu.VMEM((MM_TM, MM_TN), jnp.float32)],
        compiler_params=_cparams(("parallel", "parallel", "arbitrary")),
        name=name,
    )(a, w)


def _ffn_up_kernel(a_ref, wg_ref, wv_ref, cwg_ref, cwv_ref, cbg_ref, cbv_ref, o_ref,
                   ug_ref, uv_ref):
    i = pl.program_id(1)
    tm = a_ref.shape[0]

    @pl.when(i % (SEQ // tm) == 0)
    def _():
        ug_ref[0:SUBLANES, :] = jnp.zeros((SUBLANES, ug_ref.shape[1]), jnp.float32)
        uv_ref[0:SUBLANES, :] = jnp.zeros((SUBLANES, uv_ref.shape[1]), jnp.float32)

    a = a_ref[...]
    ug_ref[SUBLANES:, :] = jnp.dot(a, wg_ref[...], preferred_element_type=jnp.float32)
    uv_ref[SUBLANES:, :] = jnp.dot(a, wv_ref[...], preferred_element_type=jnp.float32)

    def conv(u_ref, cw_ref, cb_ref):
        cw = cw_ref[...]
        return (cw[2:3, :] * u_ref[pl.ds(SUBLANES, tm), :]
                + cw[1:2, :] * u_ref[pl.ds(SUBLANES - 1, tm), :]
                + cw[0:1, :] * u_ref[pl.ds(SUBLANES - 2, tm), :]
                + cb_ref[...])

    gate = conv(ug_ref, cwg_ref, cbg_ref)
    val = conv(uv_ref, cwv_ref, cbv_ref)
    o_ref[...] = (gate * jax.nn.sigmoid(gate) * val).astype(o_ref.dtype)

    ug_ref[0:SUBLANES, :] = ug_ref[tm:tm + SUBLANES, :]
    uv_ref[0:SUBLANES, :] = uv_ref[tm:tm + SUBLANES, :]


def _ffn_up(a, w_up, conv_w, conv_b):
    t, k = a.shape
    nj = D_FF // FFN_TN
    cb = conv_b.reshape(1, 2 * D_FF)
    return pl.pallas_call(
        _ffn_up_kernel,
        grid=(nj, t // FFN_TM),
        in_specs=[pl.BlockSpec((FFN_TM, k), lambda j, i: (i, 0)),
                  pl.BlockSpec((k, FFN_TN), lambda j, i: (0, j)),
                  pl.BlockSpec((k, FFN_TN), lambda j, i: (0, j + nj)),
                  pl.BlockSpec((3, FFN_TN), lambda j, i: (0, j)),
                  pl.BlockSpec((3, FFN_TN), lambda j, i: (0, j + nj)),
                  pl.BlockSpec((1, FFN_TN), lambda j, i: (0, j)),
                  pl.BlockSpec((1, FFN_TN), lambda j, i: (0, j + nj))],
        out_specs=pl.BlockSpec((FFN_TM, FFN_TN), lambda j, i: (i, j)),
        out_shape=jax.ShapeDtypeStruct((t, D_FF), jnp.bfloat16),
        scratch_shapes=[pltpu.VMEM((FFN_TM + SUBLANES, FFN_TN), jnp.float32),
                        pltpu.VMEM((FFN_TM + SUBLANES, FFN_TN), jnp.float32)],
        compiler_params=_cparams(("parallel", "arbitrary")),
        name="ffn_up_conv",
    )(a, w_up, w_up, conv_w, conv_w, cb, cb)


def _log_sigmoid(x):
    return jnp.minimum(x, 0.0) - jnp.log1p(jnp.exp(-jnp.abs(x)))


def _mlstm_kernel(q_ref, k_ref, v_ref, o_ref, g_ref, gb_ref, y_ref, c_ref, n_ref, m_ref):
    h = pl.program_id(1)
    c = pl.program_id(2)
    L = q_ref.shape[0]

    @pl.when(c == 0)
    def _():
        c_ref[...] = jnp.zeros_like(c_ref)
        n_ref[...] = jnp.zeros_like(n_ref)
        m_ref[...] = jnp.zeros_like(m_ref)

    g = g_ref[...] + gb_ref[...]
    gt = g.T
    lane = lax.broadcasted_iota(jnp.int32, g.shape, 1)
    sub = lax.broadcasted_iota(jnp.int32, gt.shape, 0)
    i_col = jnp.sum(jnp.where(lane == h, g, 0.0), axis=1, keepdims=True)
    f_col = jnp.sum(jnp.where(lane == h + M_HEADS, g, 0.0), axis=1, keepdims=True)
    i_row = jnp.sum(jnp.where(sub == h, gt, 0.0), axis=0, keepdims=True)
    f_row = jnp.sum(jnp.where(sub == h + M_HEADS, gt, 0.0), axis=0, keepdims=True)
    lf_col = _log_sigmoid(f_col)
    lf_row = _log_sigmoid(f_row)

    t_idx = lax.broadcasted_iota(jnp.int32, (L, L), 0)
    s_idx = lax.broadcasted_iota(jnp.int32, (L, L), 1)
    causal = s_idx <= t_idx
    b_col = jnp.sum(jnp.where(causal, lf_row, 0.0), axis=1, keepdims=True)
    b_row = jnp.sum(jnp.where(t_idx <= s_idx, lf_col, 0.0), axis=0, keepdims=True)

    m_prev = m_ref[...]
    dmat = jnp.where(causal, b_col - b_row + i_row, -jnp.inf)
    inter = b_col + m_prev
    m_t = jnp.maximum(inter, jnp.max(dmat, axis=1, keepdims=True))

    q = q_ref[...]
    ks = k_ref[...] * jnp.asarray(M_QK_DIM ** -0.5, k_ref.dtype)
    v = v_ref[...]
    qk = lax.dot_general(q, ks, (((1,), (1,)), ((), ())), preferred_element_type=jnp.float32)
    s = qk * jnp.exp(dmat - m_t)
    a = jnp.exp(inter - m_t)
    q_c = jnp.dot(q, c_ref[...].astype(jnp.bfloat16), preferred_element_type=jnp.float32)
    num = jnp.dot(s.astype(jnp.bfloat16), v, preferred_element_type=jnp.float32) + a * q_c
    q_n = jnp.sum(q.astype(jnp.float32) * n_ref[...], axis=1, keepdims=True)
    den = jnp.sum(s, axis=1, keepdims=True) + a * q_n
    hs = num * (1.0 / jnp.maximum(jnp.abs(den), jnp.exp(-m_t)))
    y_ref[...] = (jax.nn.sigmoid(o_ref[...].astype(jnp.float32)) * hs).astype(y_ref.dtype)

    b_last = b_col[L - 1:L, :]
    ws = b_last - b_col + i_col
    m_new = jnp.maximum(b_last + m_prev, jnp.max(ws, axis=0, keepdims=True))
    decay = jnp.exp(b_last + m_prev - m_new)
    kw = ks.astype(jnp.float32) * jnp.exp(ws - m_new)
    c_ref[...] = decay * c_ref[...] + jnp.dot(kw.T.astype(jnp.bfloat16), v,
                                              preferred_element_type=jnp.float32)
    n_ref[...] = decay * n_ref[...] + jnp.sum(kw, axis=0, keepdims=True)
    m_ref[...] = m_new


def _mlstm(p, gates, gate_b):
    L = MLSTM_CHUNK
    nc = SEQ // L
    kq = M_QK // M_QK_DIM
    kv = 2 * M_QK // M_V_DIM
    ko = kv + M_HEADS
    return pl.pallas_call(
        _mlstm_kernel,
        grid=(BATCH, M_HEADS, nc),
        in_specs=[pl.BlockSpec((L, M_QK_DIM), lambda b, h, c: (b * nc + c, h)),
                  pl.BlockSpec((L, M_QK_DIM), lambda b, h, c: (b * nc + c, kq + h)),
                  pl.BlockSpec((L, M_V_DIM), lambda b, h, c: (b * nc + c, kv + h)),
                  pl.BlockSpec((L, M_V_DIM), lambda b, h, c: (b * nc + c, ko + h)),
                  pl.BlockSpec((L, LANES), lambda b, h, c: (b * nc + c, 0)),
                  pl.BlockSpec((1, LANES), lambda b, h, c: (0, 0))],
        out_specs=pl.BlockSpec((L, M_V_DIM), lambda b, h, c: (b * nc + c, h)),
        out_shape=jax.ShapeDtypeStruct((TOKENS, M_V), jnp.bfloat16),
        scratch_shapes=[pltpu.VMEM((M_QK_DIM, M_V_DIM), jnp.float32),
                        pltpu.VMEM((1, M_QK_DIM), jnp.float32),
                        pltpu.VMEM((1, 1), jnp.float32)],
        compiler_params=_cparams(("parallel", "parallel", "arbitrary")),
        name="mlstm",
    )(p, p, p, p, gates, gate_b)


def _diff_attn_kernel(q_ref, k_ref, v_ref, lq1_ref, lk1_ref, lq2_ref, lk2_ref, sg_ref, o_ref,
                      acc_ref, m_ref, l_ref):
    h = pl.program_id(1)
    qi = pl.program_id(2)
    tq = q_ref.shape[0]
    tk = ATT_TK
    d = A_HEAD_DIM

    slope = jnp.exp2(jnp.full((1, 1), h + 1, jnp.int32).astype(jnp.float32) * (-8.0 / A_HEADS))
    t_idx = lax.broadcasted_iota(jnp.int32, (tq, tk), 0)
    s_idx = lax.broadcasted_iota(jnp.int32, (tq, tk), 1)
    rel = (s_idx - t_idx).astype(jnp.float32)
    srel = slope * rel

    scale = d ** -0.5
    qs = [(q_ref[:, j * d:(j + 1) * d].astype(jnp.float32) * scale).astype(jnp.bfloat16)
          for j in range(2)]

    acc_ref[...] = jnp.zeros_like(acc_ref)
    m_ref[...] = jnp.full(m_ref.shape, NEG_BIG, jnp.float32)
    l_ref[...] = jnp.zeros_like(l_ref)

    def block(kb, masked):
        k0 = pl.multiple_of(kb * tk, tk)
        kblk = k_ref[pl.ds(k0, tk), :]
        vblk = v_ref[pl.ds(k0, tk), :]
        off = slope * jnp.full((1, 1), (kb - qi) * tk, jnp.int32).astype(jnp.float32)
        for j in range(2):
            s = lax.dot_general(qs[j], kblk[:, j * d:(j + 1) * d], (((1,), (1,)), ((), ())),
                                preferred_element_type=jnp.float32) + srel
            if masked:
                s = jnp.where(rel <= 0.0, s, NEG_BIG)
            m_old = m_ref[j]
            m_new = jnp.maximum(m_old, jnp.max(s, axis=1, keepdims=True) + off)
            p = jnp.exp(s - (m_new - off))
            alpha = jnp.exp(m_old - m_new)
            l_ref[j] = alpha * l_ref[j] + jnp.sum(p, axis=1, keepdims=True)
            acc_ref[j] = alpha * acc_ref[j] + jnp.dot(p.astype(jnp.bfloat16), vblk,
                                                      preferred_element_type=jnp.float32)
            m_ref[j] = m_new

    def body(kb, carry):
        block(kb, False)
        return carry

    lax.fori_loop(0, qi, body, 0)
    block(qi, True)

    lam = (jnp.exp(jnp.sum(lq1_ref[...] * lk1_ref[...], axis=1, keepdims=True))
           - jnp.exp(jnp.sum(lq2_ref[...] * lk2_ref[...], axis=1, keepdims=True)) + LAMBDA_INIT)
    o = acc_ref[0] * (1.0 / l_ref[0]) - lam * (acc_ref[1] * (1.0 / l_ref[1]))
    o = _rms(o, sg_ref[...], SUBLN_EPS) * (1.0 - LAMBDA_INIT)
    o_ref[...] = o.astype(o_ref.dtype)


def _diff_attn(p, lq1, lk1, lq2, lk2, subln_g):
    nq = SEQ // ATT_TQ
    hw = 2 * A_HEAD_DIM
    vec = pl.BlockSpec((1, A_HEAD_DIM), lambda b, h, qi: (0, 0))
    return pl.pallas_call(
        _diff_attn_kernel,
        grid=(BATCH, A_HEADS, nq),
        in_specs=[pl.BlockSpec((ATT_TQ, hw), lambda b, h, qi: (b * nq + qi, h)),
                  pl.BlockSpec((SEQ, hw), lambda b, h, qi: (b, A_HEADS + h)),
                  pl.BlockSpec((SEQ, hw), lambda b, h, qi: (b, 2 * A_HEADS + h)),
                  vec, vec, vec, vec,
                  pl.BlockSpec((1, hw), lambda b, h, qi: (0, 0))],
        out_specs=pl.BlockSpec((ATT_TQ, hw), lambda b, h, qi: (b * nq + qi, h)),
        out_shape=jax.ShapeDtypeStruct((TOKENS, A_HD), jnp.bfloat16),
        scratch_shapes=[pltpu.VMEM((2, ATT_TQ, hw), jnp.float32),
                        pltpu.VMEM((2, ATT_TQ, 1), jnp.float32),
                        pltpu.VMEM((2, ATT_TQ, 1), jnp.float32)],
        compiler_params=_cparams(("parallel", "parallel", "arbitrary")),
        name="diff_attn",
    )(p, p, p, lq1.reshape(1, -1), lk1.reshape(1, -1), lq2.reshape(1, -1), lk2.reshape(1, -1),
      subln_g.reshape(1, -1))


def _conv_ffn(hn, w_up, conv_w, conv_b, w_down):
    act = _ffn_up(hn, w_up.astype(jnp.bfloat16), conv_w, conv_b)
    return _matmul_acc(act, w_down.astype(jnp.bfloat16), jnp.float32, name="ffn_down")


def kernel(x, l0_mix_pre_g, l0_mix_post_g, l0_w_in, l0_gate_b, l0_w_out, l0_ffn_pre_g, l0_ffn_post_g, l0_w_up, l0_conv_w, l0_conv_b, l0_w_down, l1_mix_pre_g, l1_mix_post_g, l1_w_in, l1_lambda_q1, l1_lambda_k1, l1_lambda_q2, l1_lambda_k2, l1_subln_g, l1_w_out, l1_ffn_pre_g, l1_ffn_post_g, l1_w_up, l1_conv_w, l1_conv_b, l1_w_down):
    bf16 = jnp.bfloat16
    h0 = x.reshape(TOKENS, D_MODEL)

    hn = _norm_cast(h0, l0_mix_pre_g)
    w_main = l0_w_in[:, :M_MAIN].astype(bf16)
    n_gate = 2 * M_HEADS
    w_gate = jnp.pad(l0_w_in[:, M_MAIN:], ((0, 0), (0, LANES - n_gate))).astype(bf16)
    gate_b = jnp.pad(l0_gate_b, (0, LANES - n_gate)).reshape(1, LANES)
    p = _matmul(hn, w_main, bf16, name="mlstm_in_proj")
    gates = _matmul(hn, w_gate, jnp.float32, name="mlstm_gate_proj")
    y = _mlstm(p, gates, gate_b)
    z = _matmul(y, l0_w_out.astype(bf16), jnp.float32, name="mlstm_out_proj")
    h1, hn = _residual_norm(z, h0, l0_mix_post_g, l0_ffn_pre_g)
    z = _conv_ffn(hn, l0_w_up, l0_conv_w, l0_conv_b, l0_w_down)
    h2, hn = _residual_norm(z, h1, l0_ffn_post_g, l1_mix_pre_g)

    p = _matmul(hn, l1_w_in.astype(bf16), bf16, name="attn_in_proj")
    y = _diff_attn(p, l1_lambda_q1, l1_lambda_k1, l1_lambda_q2, l1_lambda_k2, l1_subln_g)
    z = _matmul(y, l1_w_out.astype(bf16), jnp.float32, name="attn_out_proj")
    h3, hn = _residual_norm(z, h2, l1_mix_post_g, l1_ffn_pre_g)
    z = _conv_ffn(hn, l1_w_up, l1_conv_w, l1_conv_b, l1_w_down)
    h4 = _residual(z, h3, l1_ffn_post_g)
    return h4.reshape(BATCH, SEQ, D_MODEL)
```
